```python
import math
import jax, jax.numpy as jnp
from jax import lax
import numpy as np

D_MODEL = 1024
BATCH = 4
SEQ = 8192
DEPTH = 1

CHUNK = 64
MEM_LEN = 256
W_A = D_MODEL
CONV_A = 3
W_B = D_MODEL
CONV_B = 31
W_X = D_MODEL
N_HEADS_X = 4
HEAD_DIM_X = W_X // N_HEADS_X
EPS = 1e-6

SPLIT_SIZES = (
    W_A, W_A, W_A, W_A,
    W_B, W_B, W_B,
    W_X, W_X,
    D_MODEL, D_MODEL, D_MODEL,
)
IN_COLS = sum(SPLIT_SIZES)
SPLIT_POINTS = tuple(int(v) for v in np.cumsum(SPLIT_SIZES)[:-1])

kernel_name = "hybrid_shortconv_conformer_memxattn_gated"


def rmsnorm(x, g):
    xf = x.astype(jnp.float32)
    y = xf * lax.rsqrt(jnp.mean(xf * xf, axis=-1, keepdims=True) + EPS)
    return (y * g.astype(jnp.float32)).astype(x.dtype)


def layernorm(x, g, b):
    xf = x.astype(jnp.float32)
    mu = jnp.mean(xf, axis=-1, keepdims=True)
    xc = xf - mu
    var = jnp.mean(xc * xc, axis=-1, keepdims=True)
    y = xc * lax.rsqrt(var + EPS) * g.astype(jnp.float32) + b.astype(jnp.float32)
    return y.astype(x.dtype)


def causal_dwconv(x, w):
    k = w.shape[0]
    return lax.conv_general_dilated(
        x, w.astype(x.dtype)[:, None, :],
        window_strides=(1,), padding=[(k - 1, 0)],
        dimension_numbers=("NWC", "WIO", "NWC"),
        feature_group_count=x.shape[-1])


def memory_cross_attention(q, mem_n, w_kv):
    b, s, _ = q.shape
    kv = jnp.einsum("bmd,de->bme", mem_n, w_kv)
    k, v = jnp.split(kv, 2, axis=-1)
    qh = q.reshape(b, s, N_HEADS_X, HEAD_DIM_X)
    kh = k.reshape(b, -1, N_HEADS_X, HEAD_DIM_X)
    vh = v.reshape(b, -1, N_HEADS_X, HEAD_DIM_X)
    scores = jnp.einsum("bshd,bmhd->bhsm", qh, kh).astype(jnp.float32) * (HEAD_DIM_X ** -0.5)
    probs = jax.nn.softmax(scores, axis=-1).astype(q.dtype)
    o = jnp.einsum("bhsm,bmhd->bshd", probs, vh)
    return o.reshape(b, s, W_X)


def setup_inputs(seed: int = 0) -> dict:
    key = jax.random.key(seed)
    ks = jax.random.split(key, 20)
    f32 = jnp.float32
    nrm = lambda k, shape: jax.random.normal(k, shape, f32)
    return {
        "x": nrm(ks[0], (BATCH, SEQ, D_MODEL)),
        "mem": nrm(ks[1], (BATCH, MEM_LEN, D_MODEL)),
        "norm_g": 1.0 + 0.02 * nrm(ks[2], (DEPTH, D_MODEL)),
        "w_in": nrm(ks[3], (DEPTH, D_MODEL, IN_COLS)) * D_MODEL ** -0.5,
        "conv_a_w": nrm(ks[4], (DEPTH, CONV_A, W_A)) * CONV_A ** -0.5,
        "w_out_a": nrm(ks[5], (DEPTH, W_A, D_MODEL)) * W_A ** -0.5,
        "conv_b_w": nrm(ks[6], (DEPTH, CONV_B, W_B)) * CONV_B ** -0.5,
        "conv_b_b": 0.02 * nrm(ks[7], (DEPTH, W_B)),
        "ln_b_g": 1.0 + 0.02 * nrm(ks[8], (DEPTH, W_B)),
        "ln_b_b": 0.02 * nrm(ks[9], (DEPTH, W_B)),
        "w_out_b": nrm(ks[10], (DEPTH, W_B, D_MODEL)) * W_B ** -0.5,
        "mem_norm_g": 1.0 + 0.02 * nrm(ks[11], (DEPTH, D_MODEL)),
        "w_kv": nrm(ks[12], (DEPTH, D_MODEL, 2 * W_X)) * D_MODEL ** -0.5,
        "w_out_x": nrm(ks[13], (DEPTH, W_X, D_MODEL)) * W_X ** -0.5,
        "w_o": nrm(ks[14], (DEPTH, D_MODEL, D_MODEL)) * D_MODEL ** -0.5,
        "final_g": 1.0 + 0.02 * nrm(ks[15], (D_MODEL,)),
    }


def reference(x, mem, norm_g, w_in, conv_a_w, w_out_a, conv_b_w, conv_b_b,
              ln_b_g, ln_b_b, w_out_b, mem_norm_g, w_kv, w_out_x, w_o, final_g):
    h = x
    for l in range(DEPTH):
        u = rmsnorm(h, norm_g[l])
        proj = jnp.einsum("bsd,de->bse", u, w_in[l])
        (b_a, c_a, xin_a, z_a,
         val_b, gate_b, z_b,
         q_x, z_x,
         g_a, g_b, g_x) = jnp.split(proj, SPLIT_POINTS, axis=-1)

        ya = b_a * causal_dwconv(c_a * xin_a, conv_a_w[l])
        ya = jnp.einsum("bsc,cd->bsd", jax.nn.silu(z_a) * ya, w_out_a[l])

        yb = val_b * jax.nn.sigmoid(gate_b)
        yb = causal_dwconv(yb, conv_b_w[l]) + conv_b_b[l].astype(yb.dtype)
        yb = jax.nn.silu(layernorm(yb, ln_b_g[l], ln_b_b[l]))
        yb = jnp.einsum("bsc,cd->bsd", jax.nn.silu(z_b) * yb, w_out_b[l])

        mem_n = rmsnorm(mem, mem_norm_g[l])
        yx = memory_cross_attention(q_x, mem_n, w_kv[l])
        yx = jnp.einsum("bsc,cd->bsd", jax.nn.silu(z_x) * yx, w_out_x[l])

        merged = jax.nn.sigmoid(g_a) * ya + jax.nn.sigmoid(g_b) * yb + jax.nn.sigmoid(g_x) * yx
        h = h + jnp.einsum("bsd,de->bse", merged, w_o[l])
    return rmsnorm(h, final_g)
```

```python
import functools

import jax
import jax.numpy as jnp
from jax import lax
from jax.experimental import pallas as pl
from jax.experimental.pallas import tpu as pltpu

EPS = 1e-6
N_HEADS = 4
HALO_A = 8
HALO_B = 32
SEQ_TILE = 256
VMEM_LIMIT_BYTES = 56 * 1024 * 1024


def _sigmoid(v):
    return 1.0 / (1.0 + jnp.exp(-v))


def _silu(v):
    return v * _sigmoid(v)


def _rms(v, g):
    ms = jnp.mean(v * v, axis=-1, keepdims=True)
    return v * lax.rsqrt(ms + EPS) * g


def _kv_kernel(mem_ref, g_ref, wkv_ref, kt_ref, v_ref, *, width, scale):
    mem_n = _rms(mem_ref[0], g_ref[...]).astype(jnp.bfloat16)
    kv = jnp.dot(mem_n, wkv_ref[...], preferred_element_type=jnp.float32)
    k = kv[:, :width] * scale
    kt_ref[0] = k.T.astype(jnp.bfloat16)
    v_ref[0] = kv[:, width:].astype(jnp.bfloat16)


def _causal_dwconv(ext_ref, w_ref, halo, taps, tm):
    acc = None
    for k in range(taps):
        start = halo - (taps - 1) + k
        term = ext_ref[pl.ds(start, tm), :] * w_ref[pl.ds(k, 1), :]
        acc = term if acc is None else acc + term
    return acc


def _block_kernel(x_ref, kt_ref, v_ref, norm_g_ref, w_in_ref, conv_a_w_ref, w_out_a_ref,
                  conv_b_w_ref, conv_b_b_ref, ln_g_ref, ln_b_ref, w_out_b_ref, w_out_x_ref,
                  w_o_ref, final_g_ref, o_ref, cx_ext, glu_ext, *, tm, d, taps_a, taps_b):
    s = pl.program_id(1)

    @pl.when(s == 0)
    def _():
        cx_ext[pl.ds(0, HALO_A), :] = jnp.zeros((HALO_A, d), jnp.float32)
        glu_ext[pl.ds(0, HALO_B), :] = jnp.zeros((HALO_B, d), jnp.float32)

    x = x_ref[0]
    u = _rms(x, norm_g_ref[...]).astype(jnp.bfloat16)

    def proj(group):
        return jnp.dot(u, w_in_ref[:, group * d:(group + 1) * d],
                       preferred_element_type=jnp.float32)

    def out_proj(v, w_ref):
        return jnp.dot(v.astype(jnp.bfloat16), w_ref[...], preferred_element_type=jnp.float32)

    cx_ext[pl.ds(HALO_A, tm), :] = proj(1) * proj(2)
    conv_a = _causal_dwconv(cx_ext, conv_a_w_ref, HALO_A, taps_a, tm)
    cx_ext[pl.ds(0, HALO_A), :] = cx_ext[pl.ds(tm, HALO_A), :]
    ya = out_proj(_silu(proj(3)) * proj(0) * conv_a, w_out_a_ref)
    merged = _sigmoid(proj(9)) * ya

    glu_ext[pl.ds(HALO_B, tm), :] = proj(4) * _sigmoid(proj(5))
    conv_b = _causal_dwconv(glu_ext, conv_b_w_ref, HALO_B, taps_b, tm) + conv_b_b_ref[...]
    glu_ext[pl.ds(0, HALO_B), :] = glu_ext[pl.ds(tm, HALO_B), :]
    mu = jnp.mean(conv_b, axis=-1, keepdims=True)
    xc = conv_b - mu
    var = jnp.mean(xc * xc, axis=-1, keepdims=True)
    ln = xc * lax.rsqrt(var + EPS) * ln_g_ref[...] + ln_b_ref[...]
    yb = out_proj(_silu(proj(6)) * _silu(ln), w_out_b_ref)
    merged = merged + _sigmoid(proj(10)) * yb

    q = proj(7).astype(jnp.bfloat16)
    hd = d // N_HEADS
    heads = []
    for h in range(N_HEADS):
        sc = jnp.dot(q[:, h * hd:(h + 1) * hd], kt_ref[0, h * hd:(h + 1) * hd, :],
                     preferred_element_type=jnp.float32)
        e = jnp.exp(sc - jnp.max(sc, axis=-1, keepdims=True))
        p = e / jnp.sum(e, axis=-1, keepdims=True)
        heads.append(jnp.dot(p.astype(jnp.bfloat16), v_ref[0, :, h * hd:(h + 1) * hd],
                             preferred_element_type=jnp.float32))
    attn = jnp.concatenate(heads, axis=-1)
    yx = out_proj(_silu(proj(8)) * attn, w_out_x_ref)
    merged = merged + _sigmoid(proj(11)) * yx

    h_new = x + out_proj(merged, w_o_ref)
    o_ref[0] = _rms(h_new, final_g_ref[...])


def _resident(shape):
    nd = len(shape)
    return pl.BlockSpec(shape, lambda b, s: (0,) * nd, pipeline_mode=pl.Buffered(1))


def kernel(x, mem, norm_g, w_in, conv_a_w, w_out_a, conv_b_w, conv_b_b, ln_b_g, ln_b_b,
           w_out_b, mem_norm_g, w_kv, w_out_x, w_o, final_g):
    batch, seq, d = x.shape
    mem_len = mem.shape[1]
    depth = w_in.shape[0]
    width_x = w_out_x.shape[1]
    taps_a = conv_a_w.shape[1]
    taps_b = conv_b_w.shape[1]
    tm = SEQ_TILE
    assert seq % tm == 0 and taps_a - 1 <= HALO_A and taps_b - 1 <= HALO_B
    assert depth == 1
    bf16 = jnp.bfloat16
    scale = (width_x // N_HEADS) ** -0.5

    h = x
    for l in range(depth):
        kt, v = pl.pallas_call(
            functools.partial(_kv_kernel, width=width_x, scale=scale),
            grid=(batch,),
            in_specs=[
                pl.BlockSpec((1, mem_len, d), lambda b: (b, 0, 0)),
                pl.BlockSpec((1, d), lambda b: (0, 0)),
                pl.BlockSpec((d, 2 * width_x), lambda b: (0, 0)),
            ],
            out_specs=[
                pl.BlockSpec((1, width_x, mem_len), lambda b: (b, 0, 0)),
                pl.BlockSpec((1, mem_len, width_x), lambda b: (b, 0, 0)),
            ],
            out_shape=[
                jax.ShapeDtypeStruct((batch, width_x, mem_len), bf16),
                jax.ShapeDtypeStruct((batch, mem_len, width_x), bf16),
            ],
            compiler_params=pltpu.CompilerParams(dimension_semantics=("arbitrary",)),
            name="mem_kv",
        )(mem, mem_norm_g[l][None], w_kv[l].astype(bf16))

        h = pl.pallas_call(
            functools.partial(_block_kernel, tm=tm, d=d, taps_a=taps_a, taps_b=taps_b),
            grid=(batch, seq // tm),
            in_specs=[
                pl.BlockSpec((1, tm, d), lambda b, s: (b, s, 0)),
                pl.BlockSpec((1, width_x, mem_len), lambda b, s: (b, 0, 0)),
                pl.BlockSpec((1, mem_len, width_x), lambda b, s: (b, 0, 0)),
                _resident((1, d)),
                _resident(w_in.shape[1:]),
                _resident(conv_a_w.shape[1:]),
                _resident(w_out_a.shape[1:]),
                _resident(conv_b_w.shape[1:]),
                _resident((1, d)),
                _resident((1, d)),
                _resident((1, d)),
                _resident(w_out_b.shape[1:]),
                _resident(w_out_x.shape[1:]),
                _resident(w_o.shape[1:]),
                _resident((1, d)),
            ],
            out_specs=pl.BlockSpec((1, tm, d), lambda b, s: (b, s, 0)),
            out_shape=jax.ShapeDtypeStruct((batch, seq, d), x.dtype),
            scratch_shapes=[
                pltpu.VMEM((HALO_A + tm, d), jnp.float32),
                pltpu.VMEM((HALO_B + tm, d), jnp.float32),
            ],
            compiler_params=pltpu.CompilerParams(
                dimension_semantics=("arbitrary", "arbitrary"),
                vmem_limit_bytes=VMEM_LIMIT_BYTES),
            name="fused_block",
        )(h, kt, v, norm_g[l][None], w_in[l].astype(bf16), conv_a_w[l],
          w_out_a[l].astype(bf16), conv_b_w[l], conv_b_b[l][None], ln_b_g[l][None],
          ln_b_b[l][None], w_out_b[l].astype(bf16), w_out_x[l].astype(bf16),
          w_o[l].astype(bf16), final_g[None])
    return h
```

```python
import functools
import math

import jax
import jax.numpy as jnp
from jax import lax
from jax.experimental import pallas as pl
from jax.experimental.pallas import tpu as pltpu

EPS = 1e-6
N_HEADS = 4
SUBLANES = 8
LANES = 128
HALO_A = 8
HALO_B = 32
CONV_ROWS = 64
SEQ_TILE = 256
VMEM_LIMIT_BYTES = 56 * 1024 * 1024
NEG_LOG2_E = -math.log2(math.e)


def _sigmoid(v):
    return 1.0 / (1.0 + jnp.exp2(v * NEG_LOG2_E))


def _silu(v):
    return v * _sigmoid(v)


def _rms(v, g):
    ms = jnp.mean(v * v, axis=-1, keepdims=True)
    return v * lax.rsqrt(ms + EPS) * g


def _pack_rows(w):
    k, n = w.shape
    wb = w.astype(jnp.bfloat16).reshape(k // 2, 2, n).transpose(0, 2, 1)
    return lax.bitcast_convert_type(wb, jnp.uint32)


def _repeat_sublanes(w):
    return jnp.broadcast_to(w[:, None, :], (w.shape[0], SUBLANES, w.shape[1]))


def _as_bf16(words):
    return pltpu.bitcast(words, jnp.bfloat16)


def _kv_kernel(mem_ref, g_ref, wkv_ref, kt_ref, v_ref, *, width, scale):
    mem_n = _rms(mem_ref[0], g_ref[...]).astype(jnp.bfloat16)
    kv = jnp.dot(mem_n, _as_bf16(wkv_ref[...]), preferred_element_type=jnp.float32)
    k = kv[:, :width] * scale
    kt_ref[0] = pltpu.bitcast(k.T.astype(jnp.bfloat16), jnp.uint32)
    v_ref[0] = pltpu.bitcast(kv[:, width:].astype(jnp.bfloat16), jnp.uint32)


def _causal_dwconv(ext_ref, w_ref, out_ref, cols, halo, tm):
    taps = w_ref.shape[0]
    c = cols.size
    gt = CONV_ROWS // SUBLANES
    ht = halo // SUBLANES
    sub = lax.broadcasted_iota(jnp.int32, (1, SUBLANES, c), 1)
    for g in range(tm // CONV_ROWS):
        e3 = ext_ref[pl.ds(g * CONV_ROWS, halo + CONV_ROWS), cols].reshape(ht + gt, SUBLANES, c)
        acc = None
        rot = e3
        for r in range(min(SUBLANES, taps)):
            if r == 0:
                s3 = e3[1:]
            else:
                rot = pltpu.roll(rot, 1, axis=1)
                s3 = jnp.where(sub < r, rot[:-1], rot[1:])
            for a in range((taps - 1 - r) // SUBLANES + 1):
                k = taps - 1 - (SUBLANES * a + r)
                j0 = ht - 1 - a
                term = s3[j0:j0 + gt] * w_ref[k, :, cols][None]
                acc = term if acc is None else acc + term
        out_ref[pl.ds(g * CONV_ROWS, CONV_ROWS), cols] = acc.reshape(CONV_ROWS, c)


def _block_kernel(x_ref, kt_ref, v_ref, norm_g_ref, w_in_ref, conv_a_w_ref, w_out_a_ref,
                  conv_b_w_ref, conv_b_b_ref, ln_g_ref, ln_b_ref, w_out_b_ref, w_out_x_ref,
                  w_o_ref, final_g_ref, o_ref, cx_ext, glu_ext, conv_a_out, conv_b_out, *, tm, d):
    s = pl.program_id(1)

    @pl.when(s == 0)
    def _():
        cx_ext[pl.ds(0, HALO_A), :] = jnp.zeros((HALO_A, d), jnp.float32)
        glu_ext[pl.ds(0, HALO_B), :] = jnp.zeros((HALO_B, d), jnp.float32)

    x = x_ref[0]
    u = _rms(x, norm_g_ref[...]).astype(jnp.bfloat16)

    def proj(group):
        w = _as_bf16(w_in_ref[:, group * d:(group + 1) * d])
        return jnp.dot(u, w, preferred_element_type=jnp.float32)

    def out_proj(v, w_ref):
        return jnp.dot(v.astype(jnp.bfloat16), _as_bf16(w_ref[...]),
                       preferred_element_type=jnp.float32)

    conv_b_pieces = iter(range(d // LANES))

    def conv_b_piece():
        cols = pl.ds(next(conv_b_pieces) * LANES, LANES)
        _causal_dwconv(glu_ext, conv_b_w_ref, conv_b_out, cols, HALO_B, tm)

    glu_ext[pl.ds(HALO_B, tm), :] = proj(4) * _sigmoid(proj(5))

    szb = _silu(proj(6))
    conv_b_piece()
    sgb = _sigmoid(proj(10))
    conv_b_piece()

    cx_ext[pl.ds(HALO_A, tm), :] = proj(1) * proj(2)
    conv_b_piece()
    for c in range(d // LANES):
        _causal_dwconv(cx_ext, conv_a_w_ref, conv_a_out, pl.ds(c * LANES, LANES), HALO_A, tm)
    cx_ext[pl.ds(0, HALO_A), :] = cx_ext[pl.ds(tm, HALO_A), :]
    ya_in = _silu(proj(3)) * conv_a_out[...]
    conv_b_piece()
    ya = out_proj(ya_in * proj(0), w_out_a_ref)
    conv_b_piece()
    merged = _sigmoid(proj(9)) * ya
    conv_b_piece()

    q = proj(7).astype(jnp.bfloat16)
    conv_b_piece()
    hd = d // N_HEADS
    heads = []
    for h in range(N_HEADS):
        kt_h = _as_bf16(kt_ref[0, h * hd // 2:(h + 1) * hd // 2, :])
        v_h = _as_bf16(v_ref[0, :, h * hd:(h + 1) * hd])
        sc = jnp.dot(q[:, h * hd:(h + 1) * hd], kt_h, preferred_element_type=jnp.float32)
        e = jnp.exp(sc - jnp.max(sc, axis=-1, keepdims=True))
        p = e / jnp.sum(e, axis=-1, keepdims=True)
        heads.append(jnp.dot(p.astype(jnp.bfloat16), v_h, preferred_element_type=jnp.float32))
    attn = jnp.concatenate(heads, axis=-1)
    szx = _silu(proj(8))
    conv_b_piece()
    assert next(conv_b_pieces, None) is None
    glu_ext[pl.ds(0, HALO_B), :] = glu_ext[pl.ds(tm, HALO_B), :]
    yx = out_proj(szx * attn, w_out_x_ref)
    merged = merged + _sigmoid(proj(11)) * yx

    conv_b = conv_b_out[...] + conv_b_b_ref[...]
    mu = jnp.mean(conv_b, axis=-1, keepdims=True)
    xc = conv_b - mu
    var = jnp.mean(xc * xc, axis=-1, keepdims=True)
    ln = xc * lax.rsqrt(var + EPS) * ln_g_ref[...] + ln_b_ref[...]
    yb = out_proj(szb * _silu(ln), w_out_b_ref)
    merged = merged + sgb * yb

    h_new = x + out_proj(merged, w_o_ref)
    o_ref[0] = _rms(h_new, final_g_ref[...])


def _resident(shape):
    nd = len(shape)
    return pl.BlockSpec(shape, lambda b, s: (0,) * nd, pipeline_mode=pl.Buffered(1))


def kernel(x, mem, norm_g, w_in, conv_a_w, w_out_a, conv_b_w, conv_b_b, ln_b_g, ln_b_b,
           w_out_b, mem_norm_g, w_kv, w_out_x, w_o, final_g):
    batch, seq, d = x.shape
    mem_len = mem.shape[1]
    depth = w_in.shape[0]
    width_x = w_out_x.shape[1]
    taps_a = conv_a_w.shape[1]
    taps_b = conv_b_w.shape[1]
    tm = SEQ_TILE
    assert seq % tm == 0 and tm % CONV_ROWS == 0
    assert taps_a - 1 <= HALO_A and taps_b - 1 <= HALO_B
    assert depth == 1
    scale = (width_x // N_HEADS) ** -0.5
    u32 = jnp.uint32
    f32 = jnp.float32

    kt, v = pl.pallas_call(
        functools.partial(_kv_kernel, width=width_x, scale=scale),
        grid=(batch,),
        in_specs=[
            pl.BlockSpec((1, mem_len, d), lambda b: (b, 0, 0)),
            pl.BlockSpec((1, d), lambda b: (0, 0)),
            pl.BlockSpec((d // 2, 2 * width_x), lambda b: (0, 0)),
        ],
        out_specs=[
            pl.BlockSpec((1, width_x // 2, mem_len), lambda b: (b, 0, 0)),
            pl.BlockSpec((1, mem_len // 2, width_x), lambda b: (b, 0, 0)),
        ],
        out_shape=[
            jax.ShapeDtypeStruct((batch, width_x // 2, mem_len), u32),
            jax.ShapeDtypeStruct((batch, mem_len // 2, width_x), u32),
        ],
        compiler_params=pltpu.CompilerParams(dimension_semantics=("arbitrary",)),
        name="mem_kv",
    )(mem, mem_norm_g[0][None], _pack_rows(w_kv[0]))

    return pl.pallas_call(
        functools.partial(_block_kernel, tm=tm, d=d),
        grid=(batch, seq // tm),
        in_specs=[
            pl.BlockSpec((1, tm, d), lambda b, s: (b, s, 0)),
            pl.BlockSpec((1, width_x // 2, mem_len), lambda b, s: (b, 0, 0)),
            pl.BlockSpec((1, mem_len // 2, width_x), lambda b, s: (b, 0, 0)),
            _resident((1, d)),
            _resident((d // 2, w_in.shape[2])),
            _resident((taps_a, SUBLANES, d)),
            _resident((w_out_a.shape[1] // 2, d)),
            _resident((taps_b, SUBLANES, d)),
            _resident((1, d)),
            _resident((1, d)),
            _resident((1, d)),
            _resident((w_out_b.shape[1] // 2, d)),
            _resident((width_x // 2, d)),
            _resident((d // 2, d)),
            _resident((1, d)),
        ],
        out_specs=pl.BlockSpec((1, tm, d), lambda b, s: (b, s, 0)),
        out_shape=jax.ShapeDtypeStruct((batch, seq, d), x.dtype),
        scratch_shapes=[
            pltpu.VMEM((HALO_A + tm, d), f32),
            pltpu.VMEM((HALO_B + tm, d), f32),
            pltpu.VMEM((tm, d), f32),
            pltpu.VMEM((tm, d), f32),
        ],
        compiler_params=pltpu.CompilerParams(
            dimension_semantics=("arbitrary", "arbitrary"),
            vmem_limit_bytes=VMEM_LIMIT_BYTES),
        name="fused_block",
    )(x, kt, v, norm_g[0][None], _pack_rows(w_in[0]), _repeat_sublanes(conv_a_w[0]),
      _pack_rows(w_out_a[0]), _repeat_sublanes(conv_b_w[0]), conv_b_b[0][None], ln_b_g[0][None],
      ln_b_b[0][None], _pack_rows(w_out_b[0]), _pack_rows(w_out_x[0]), _pack_rows(w_o[0]),
      final_g[None])
```

```python
import functools
import math

import jax
import jax.numpy as jnp
from jax import lax
from jax.experimental import pallas as pl
from jax.experimental.pallas import tpu as pltpu

EPS = 1e-6
N_HEADS = 4
SUBLANES = 8
LANES = 128
HALO_A = 8
HALO_B = 32
CONV_ROWS = 32
SEQ_TILE = 256
PACK_STEPS_IN = 24
PACK_STEPS_OUT = 2
VMEM_LIMIT_BYTES = 56 * 1024 * 1024
NEG_LOG2_E = -math.log2(math.e)


def _sigmoid(v):
    return 1.0 / (1.0 + jnp.exp2(v * NEG_LOG2_E))


def _silu(v):
    return v * _sigmoid(v)


def _rms(v, g):
    ms = jnp.mean(v * v, axis=-1, keepdims=True)
    return v * lax.rsqrt(ms + EPS) * g


def _pack_kernel(*refs):
    n = len(refs) // 2
    for w_ref, o_ref in zip(refs[:n], refs[n:]):
        o_ref[...] = pltpu.bitcast(w_ref[...].astype(jnp.bfloat16), jnp.uint32)


def _pack_rows(ws, steps):
    return pl.pallas_call(
        _pack_kernel,
        grid=(steps,),
        in_specs=[pl.BlockSpec((w.shape[0], w.shape[1] // steps), lambda j: (0, j)) for w in ws],
        out_specs=[pl.BlockSpec((w.shape[0] // 2, w.shape[1] // steps), lambda j: (0, j)) for w in ws],
        out_shape=[jax.ShapeDtypeStruct((w.shape[0] // 2, w.shape[1]), jnp.uint32) for w in ws],
        compiler_params=pltpu.CompilerParams(dimension_semantics=("arbitrary",)),
        name="pack_rows",
    )(*ws)


def _repeat_sublanes(w):
    return jnp.broadcast_to(w[:, None, :], (w.shape[0], SUBLANES, w.shape[1]))


def _as_bf16(words):
    return pltpu.bitcast(words, jnp.bfloat16)


def _kv_kernel(mem_ref, g_ref, wkv_ref, kt_ref, v_ref, *, width, scale):
    mem_n = _rms(mem_ref[0], g_ref[...]).astype(jnp.bfloat16)
    kv = jnp.dot(mem_n, _as_bf16(wkv_ref[...]), preferred_element_type=jnp.float32)
    k = kv[:, :width] * scale
    kt_ref[0] = pltpu.bitcast(k.T.astype(jnp.bfloat16), jnp.uint32)
    v_ref[0] = pltpu.bitcast(kv[:, width:].astype(jnp.bfloat16), jnp.uint32)


def _causal_dwconv(ext_ref, w_ref, out_ref, cols, halo, tm):
    taps = w_ref.shape[0]
    c = cols.size
    gt = CONV_ROWS // SUBLANES
    ht = halo // SUBLANES
    sub = lax.broadcasted_iota(jnp.int32, (1, SUBLANES, c), 1)
    for g in range(tm // CONV_ROWS):
        e3 = ext_ref[pl.ds(g * CONV_ROWS, halo + CONV_ROWS), cols].reshape(ht + gt, SUBLANES, c)
        acc = None
        rot = e3
        for r in range(min(SUBLANES, taps)):
            if r == 0:
                s3 = e3[1:]
            else:
                rot = pltpu.roll(rot, 1, axis=1)
                s3 = jnp.where(sub < r, rot[:-1], rot[1:])
            for a in range((taps - 1 - r) // SUBLANES + 1):
                k = taps - 1 - (SUBLANES * a + r)
                j0 = ht - 1 - a
                term = s3[j0:j0 + gt] * w_ref[k, :, cols][None]
                acc = term if acc is None else acc + term
        out_ref[pl.ds(g * CONV_ROWS, CONV_ROWS), cols] = acc.reshape(CONV_ROWS, c)


def _block_kernel(x_ref, kt_ref, v_ref, norm_g_ref, w_in_ref, conv_a_w_ref, w_out_a_ref,
                  conv_b_w_ref, conv_b_b_ref, ln_g_ref, ln_b_ref, w_out_b_ref, w_out_x_ref,
                  w_o_ref, final_g_ref, o_ref, cx_ext, glu_ext, conv_a_out, conv_b_out, *, tm, d):
    s = pl.program_id(1)

    @pl.when(s == 0)
    def _():
        cx_ext[pl.ds(0, HALO_A), :] = jnp.zeros((HALO_A, d), jnp.float32)
        glu_ext[pl.ds(0, HALO_B), :] = jnp.zeros((HALO_B, d), jnp.float32)

    x = x_ref[0]
    u = _rms(x, norm_g_ref[...]).astype(jnp.bfloat16)

    def proj(group):
        w = _as_bf16(w_in_ref[:, group * d:(group + 1) * d])
        return jnp.dot(u, w, preferred_element_type=jnp.float32)

    def out_proj(v, w_ref):
        return jnp.dot(v.astype(jnp.bfloat16), _as_bf16(w_ref[...]),
                       preferred_element_type=jnp.float32)

    conv_b_pieces = iter(range(d // LANES))

    def conv_b_piece():
        cols = pl.ds(next(conv_b_pieces) * LANES, LANES)
        _causal_dwconv(glu_ext, conv_b_w_ref, conv_b_out, cols, HALO_B, tm)

    glu_ext[pl.ds(HALO_B, tm), :] = proj(4) * _sigmoid(proj(5))

    szb = _silu(proj(6))
    conv_b_piece()
    sgb = _sigmoid(proj(10))
    conv_b_piece()

    cx_ext[pl.ds(HALO_A, tm), :] = proj(1) * proj(2)
    conv_b_piece()
    for c in range(d // LANES):
        _causal_dwconv(cx_ext, conv_a_w_ref, conv_a_out, pl.ds(c * LANES, LANES), HALO_A, tm)
    cx_ext[pl.ds(0, HALO_A), :] = cx_ext[pl.ds(tm, HALO_A), :]
    ya_in = _silu(proj(3)) * conv_a_out[...]
    conv_b_piece()
    ya = out_proj(ya_in * proj(0), w_out_a_ref)
    conv_b_piece()
    merged = _sigmoid(proj(9)) * ya
    conv_b_piece()

    q = proj(7).astype(jnp.bfloat16)
    conv_b_piece()
    hd = d // N_HEADS
    heads = []
    for h in range(N_HEADS):
        kt_h = _as_bf16(kt_ref[0, h * hd // 2:(h + 1) * hd // 2, :])
        v_h = _as_bf16(v_ref[0, :, h * hd:(h + 1) * hd])
        sc = jnp.dot(q[:, h * hd:(h + 1) * hd], kt_h, preferred_element_type=jnp.float32)
        e = jnp.exp(sc - jnp.max(sc, axis=-1, keepdims=True))
        p = e / jnp.sum(e, axis=-1, keepdims=True)
        heads.append(jnp.dot(p.astype(jnp.bfloat16), v_h, preferred_element_type=jnp.float32))
    attn = jnp.concatenate(heads, axis=-1)
    szx = _silu(proj(8))
    conv_b_piece()
    assert next(conv_b_pieces, None) is None
    glu_ext[pl.ds(0, HALO_B), :] = glu_ext[pl.ds(tm, HALO_B), :]
    yx = out_proj(szx * attn, w_out_x_ref)
    merged = merged + _sigmoid(proj(11)) * yx

    conv_b = conv_b_out[...] + conv_b_b_ref[...]
    mu = jnp.mean(conv_b, axis=-1, keepdims=True)
    xc = conv_b - mu
    var = jnp.mean(xc * xc, axis=-1, keepdims=True)
    ln = xc * lax.rsqrt(var + EPS) * ln_g_ref[...] + ln_b_ref[...]
    yb = out_proj(szb * _silu(ln), w_out_b_ref)
    merged = merged + sgb * yb

    h_new = x + out_proj(merged, w_o_ref)
    o_ref[0] = _rms(h_new, final_g_ref[...])


def _resident(shape):
    nd = len(shape)
    return pl.BlockSpec(shape, lambda b, s: (0,) * nd, pipeline_mode=pl.Buffered(1))


def kernel(x, mem, norm_g, w_in, conv_a_w, w_out_a, conv_b_w, conv_b_b, ln_b_g, ln_b_b,
           w_out_b, mem_norm_g, w_kv, w_out_x, w_o, final_g):
    batch, seq, d = x.shape
    mem_len = mem.shape[1]
    depth = w_in.shape[0]
    width_x = w_out_x.shape[1]
    taps_a = conv_a_w.shape[1]
    taps_b = conv_b_w.shape[1]
    tm = SEQ_TILE
    assert seq % tm == 0 and tm % CONV_ROWS == 0
    assert taps_a - 1 <= HALO_A and taps_b - 1 <= HALO_B
    assert depth == 1
    scale = (width_x // N_HEADS) ** -0.5
    u32 = jnp.uint32
    f32 = jnp.float32

    (p_in,) = _pack_rows([w_in[0]], PACK_STEPS_IN)
    p_kv, p_out_a, p_out_b, p_out_x, p_o = _pack_rows(
        [w_kv[0], w_out_a[0], w_out_b[0], w_out_x[0], w_o[0]], PACK_STEPS_OUT)

    kt, v = pl.pallas_call(
        functools.partial(_kv_kernel, width=width_x, scale=scale),
        grid=(batch,),
        in_specs=[
            pl.BlockSpec((1, mem_len, d), lambda b: (b, 0, 0)),
            pl.BlockSpec((1, d), lambda b: (0, 0)),
            pl.BlockSpec((d // 2, 2 * width_x), lambda b: (0, 0)),
        ],
        out_specs=[
            pl.BlockSpec((1, width_x // 2, mem_len), lambda b: (b, 0, 0)),
            pl.BlockSpec((1, mem_len // 2, width_x), lambda b: (b, 0, 0)),
        ],
        out_shape=[
            jax.ShapeDtypeStruct((batch, width_x // 2, mem_len), u32),
            jax.ShapeDtypeStruct((batch, mem_len // 2, width_x), u32),
        ],
        compiler_params=pltpu.CompilerParams(dimension_semantics=("arbitrary",)),
        name="mem_kv",
    )(mem, mem_norm_g[0][None], p_kv)

    return pl.pallas_call(
        functools.partial(_block_kernel, tm=tm, d=d),
        grid=(batch, seq // tm),
        in_specs=[
            pl.BlockSpec((1, tm, d), lambda b, s: (b, s, 0)),
            pl.BlockSpec((1, width_x // 2, mem_len), lambda b, s: (b, 0, 0)),
            pl.BlockSpec((1, mem_len // 2, width_x), lambda b, s: (b, 0, 0)),
            _resident((1, d)),
            _resident((d // 2, w_in.shape[2])),
            _resident((taps_a, SUBLANES, d)),
            _resident((w_out_a.shape[1] // 2, d)),
            _resident((taps_b, SUBLANES, d)),
            _resident((1, d)),
            _resident((1, d)),
            _resident((1, d)),
            _resident((w_out_b.shape[1] // 2, d)),
            _resident((width_x // 2, d)),
            _resident((d // 2, d)),
            _resident((1, d)),
        ],
        out_specs=pl.BlockSpec((1, tm, d), lambda b, s: (b, s, 0)),
        out_shape=jax.ShapeDtypeStruct((batch, seq, d), x.dtype),
        scratch_shapes=[
            pltpu.VMEM((HALO_A + tm, d), f32),
            pltpu.VMEM((HALO_B + tm, d), f32),
            pltpu.VMEM((tm, d), f32),
            pltpu.VMEM((tm, d), f32),
        ],
        compiler_params=pltpu.CompilerParams(
            dimension_semantics=("arbitrary", "arbitrary"),
            vmem_limit_bytes=VMEM_LIMIT_BYTES),
        name="fused_block",
    )(x, kt, v, norm_g[0][None], p_in, _repeat_sublanes(conv_a_w[0]), p_out_a,
      _repeat_sublanes(conv_b_w[0]), conv_b_b[0][None], ln_b_g[0][None], ln_b_b[0][None],
      p_out_b, p_out_x, p_o, final_g[None])
```

```python
import functools
import math

import jax
import jax.numpy as jnp
from jax import lax
from jax.experimental import pallas as pl
from jax.experimental.pallas import tpu as pltpu

EPS = 1e-6
N_HEADS = 4
SUBLANES = 8
LANES = 128
HALO_A = 8
HALO_B = 32
CONV_ROWS = 16
SEQ_TILE = 256
CAST_COLS = 512
VMEM_LIMIT_BYTES = 58 * 1024 * 1024
NEG_LOG2_E = -math.log2(math.e)


def _sigmoid(v):
    return 1.0 / (1.0 + jnp.exp2(v * NEG_LOG2_E))


def _silu(v):
    return v * _sigmoid(v)


def _rms(v, g):
    ms = jnp.mean(v * v, axis=-1, keepdims=True)
    return v * lax.rsqrt(ms + EPS) * g


def _pack_rows(w):
    return pltpu.bitcast(w.astype(jnp.bfloat16), jnp.uint32)


def _stage_weight(src_hbm, dst, stage, sem):
    n_chunks = src_hbm.shape[1] // CAST_COLS

    def chunk_cols(c):
        return pl.ds(pl.multiple_of(c * CAST_COLS, CAST_COLS), CAST_COLS)

    def chunk_copy(c, slot):
        return pltpu.make_async_copy(src_hbm.at[:, chunk_cols(c)], stage.at[slot], sem.at[slot])

    chunk_copy(0, 0).start()

    def body(c, carry):
        slot = lax.rem(c, 2)

        @pl.when(c + 1 < n_chunks)
        def _():
            chunk_copy(c + 1, 1 - slot).start()

        chunk_copy(c, slot).wait()
        dst[:, chunk_cols(c)] = _pack_rows(stage[slot])
        return carry

    lax.fori_loop(0, n_chunks, body, 0)


def _repeat_sublanes(w):
    return jnp.broadcast_to(w[:, None, :], (w.shape[0], SUBLANES, w.shape[1]))


def _as_bf16(words):
    return pltpu.bitcast(words, jnp.bfloat16)


def _kv_kernel(mem_ref, g_ref, wkv_ref, kt_ref, v_ref, *, width, scale):
    mem_n = _rms(mem_ref[0], g_ref[...]).astype(jnp.bfloat16)
    kv = jnp.dot(mem_n, wkv_ref[...].astype(jnp.bfloat16),
                 preferred_element_type=jnp.float32)
    k = kv[:, :width] * scale
    kt_ref[0] = pltpu.bitcast(k.T.astype(jnp.bfloat16), jnp.uint32)
    v_ref[0] = pltpu.bitcast(kv[:, width:].astype(jnp.bfloat16), jnp.uint32)


def _causal_dwconv(ext_ref, w_ref, out_ref, cols, halo, tm):
    taps = w_ref.shape[0]
    c = cols.size
    gt = CONV_ROWS // SUBLANES
    ht = halo // SUBLANES
    sub = lax.broadcasted_iota(jnp.int32, (1, SUBLANES, c), 1)
    for g in range(tm // CONV_ROWS):
        e3 = ext_ref[pl.ds(g * CONV_ROWS, halo + CONV_ROWS), cols].reshape(ht + gt, SUBLANES, c)
        acc = None
        rot = e3
        for r in range(min(SUBLANES, taps)):
            if r == 0:
                s3 = e3[1:]
            else:
                rot = pltpu.roll(rot, 1, axis=1)
                s3 = jnp.where(sub < r, rot[:-1], rot[1:])
            for a in range((taps - 1 - r) // SUBLANES + 1):
                k = taps - 1 - (SUBLANES * a + r)
                j0 = ht - 1 - a
                term = s3[j0:j0 + gt] * w_ref[k, :, cols][None]
                acc = term if acc is None else acc + term
        out_ref[pl.ds(g * CONV_ROWS, CONV_ROWS), cols] = acc.reshape(CONV_ROWS, c)


def _block_kernel(x_ref, kt_ref, v_ref, norm_g_ref, w_in_hbm, conv_a_w_ref, w_out_a_hbm,
                  conv_b_w_ref, conv_b_b_ref, ln_g_ref, ln_b_ref, w_out_b_hbm, w_out_x_hbm,
                  w_o_hbm, final_g_ref, o_ref, w_in_ref, w_out_a_ref, w_out_b_ref, w_out_x_ref,
                  w_o_ref, stage, stage_sem, cx_ext, glu_ext, conv_a_out, conv_b_out, *, tm, d):
    s = pl.program_id(1)

    @pl.when((pl.program_id(0) == 0) & (s == 0))
    def _():
        for src, dst in ((w_in_hbm, w_in_ref), (w_out_a_hbm, w_out_a_ref), (w_out_b_hbm, w_out_b_ref),
                         (w_out_x_hbm, w_out_x_ref), (w_o_hbm, w_o_ref)):
            _stage_weight(src, dst, stage, stage_sem)

    @pl.when(s == 0)
    def _():
        cx_ext[pl.ds(0, HALO_A), :] = jnp.zeros((HALO_A, d), jnp.float32)
        glu_ext[pl.ds(0, HALO_B), :] = jnp.zeros((HALO_B, d), jnp.float32)

    x = x_ref[0]
    u = _rms(x, norm_g_ref[...]).astype(jnp.bfloat16)

    def proj(group):
        w = _as_bf16(w_in_ref[:, group * d:(group + 1) * d])
        return jnp.dot(u, w, preferred_element_type=jnp.float32)

    def out_proj(v, w_ref):
        return jnp.dot(v.astype(jnp.bfloat16), _as_bf16(w_ref[...]),
                       preferred_element_type=jnp.float32)

    conv_b_pieces = iter(range(d // LANES))

    def conv_b_piece():
        cols = pl.ds(next(conv_b_pieces) * LANES, LANES)
        _causal_dwconv(glu_ext, conv_b_w_ref, conv_b_out, cols, HALO_B, tm)

    glu_ext[pl.ds(HALO_B, tm), :] = proj(4) * _sigmoid(proj(5))

    szb = _silu(proj(6))
    conv_b_piece()
    sgb = _sigmoid(proj(10))
    conv_b_piece()

    cx_ext[pl.ds(HALO_A, tm), :] = proj(1) * proj(2)
    conv_b_piece()
    for c in range(d // LANES):
        _causal_dwconv(cx_ext, conv_a_w_ref, conv_a_out, pl.ds(c * LANES, LANES), HALO_A, tm)
    cx_ext[pl.ds(0, HALO_A), :] = cx_ext[pl.ds(tm, HALO_A), :]
    ya_in = _silu(proj(3)) * conv_a_out[...]
    conv_b_piece()
    ya = out_proj(ya_in * proj(0), w_out_a_ref)
    conv_b_piece()
    merged = _sigmoid(proj(9)) * ya
    conv_b_piece()

    q = proj(7).astype(jnp.bfloat16)
    conv_b_piece()
    hd = d // N_HEADS
    heads = []
    for h in range(N_HEADS):
        kt_h = _as_bf16(kt_ref[0, h * hd // 2:(h + 1) * hd // 2, :])
        v_h = _as_bf16(v_ref[0, :, h * hd:(h + 1) * hd])
        sc = jnp.dot(q[:, h * hd:(h + 1) * hd], kt_h, preferred_element_type=jnp.float32)
        e = jnp.exp(sc - jnp.max(sc, axis=-1, keepdims=True))
        p = e / jnp.sum(e, axis=-1, keepdims=True)
        heads.append(jnp.dot(p.astype(jnp.bfloat16), v_h, preferred_element_type=jnp.float32))
    attn = jnp.concatenate(heads, axis=-1)
    szx = _silu(proj(8))
    conv_b_piece()
    assert next(conv_b_pieces, None) is None
    glu_ext[pl.ds(0, HALO_B), :] = glu_ext[pl.ds(tm, HALO_B), :]
    yx = out_proj(szx * attn, w_out_x_ref)
    merged = merged + _sigmoid(proj(11)) * yx

    conv_b = conv_b_out[...] + conv_b_b_ref[...]
    mu = jnp.mean(conv_b, axis=-1, keepdims=True)
    xc = conv_b - mu
    var = jnp.mean(xc * xc, axis=-1, keepdims=True)
    ln = xc * lax.rsqrt(var + EPS) * ln_g_ref[...] + ln_b_ref[...]
    yb = out_proj(szb * _silu(ln), w_out_b_ref)
    merged = merged + sgb * yb

    h_new = x + out_proj(merged, w_o_ref)
    o_ref[0] = _rms(h_new, final_g_ref[...])


def _resident(shape):
    nd = len(shape)
    return pl.BlockSpec(shape, lambda b, s: (0,) * nd, pipeline_mode=pl.Buffered(1))


def kernel(x, mem, norm_g, w_in, conv_a_w, w_out_a, conv_b_w, conv_b_b, ln_b_g, ln_b_b,
           w_out_b, mem_norm_g, w_kv, w_out_x, w_o, final_g):
    batch, seq, d = x.shape
    mem_len = mem.shape[1]
    depth = w_in.shape[0]
    width_x = w_out_x.shape[1]
    taps_a = conv_a_w.shape[1]
    taps_b = conv_b_w.shape[1]
    tm = SEQ_TILE
    assert seq % tm == 0 and tm % CONV_ROWS == 0
    assert taps_a - 1 <= HALO_A and taps_b - 1 <= HALO_B
    assert depth == 1
    scale = (width_x // N_HEADS) ** -0.5
    u32 = jnp.uint32
    f32 = jnp.float32

    kt, v = pl.pallas_call(
        functools.partial(_kv_kernel, width=width_x, scale=scale),
        grid=(batch,),
        in_specs=[
            pl.BlockSpec((1, mem_len, d), lambda b: (b, 0, 0)),
            pl.BlockSpec((1, d), lambda b: (0, 0)),
            pl.BlockSpec((d, 2 * width_x), lambda b: (0, 0)),
        ],
        out_specs=[
            pl.BlockSpec((1, width_x // 2, mem_len), lambda b: (b, 0, 0)),
            pl.BlockSpec((1, mem_len // 2, width_x), lambda b: (b, 0, 0)),
        ],
        out_shape=[
            jax.ShapeDtypeStruct((batch, width_x // 2, mem_len), u32),
            jax.ShapeDtypeStruct((batch, mem_len // 2, width_x), u32),
        ],
        compiler_params=pltpu.CompilerParams(dimension_semantics=("arbitrary",)),
        name="mem_kv",
    )(mem, mem_norm_g[0][None], w_kv[0])

    return pl.pallas_call(
        functools.partial(_block_kernel, tm=tm, d=d),
        grid=(batch, seq // tm),
        in_specs=[
            pl.BlockSpec((1, tm, d), lambda b, s: (b, s, 0)),
            pl.BlockSpec((1, width_x // 2, mem_len), lambda b, s: (b, 0, 0)),
            pl.BlockSpec((1, mem_len // 2, width_x), lambda b, s: (b, 0, 0)),
            _resident((1, d)),
            pl.BlockSpec(memory_space=pl.ANY),
            _resident((taps_a, SUBLANES, d)),
            pl.BlockSpec(memory_space=pl.ANY),
            _resident((taps_b, SUBLANES, d)),
            _resident((1, d)),
            _resident((1, d)),
            _resident((1, d)),
            pl.BlockSpec(memory_space=pl.ANY),
            pl.BlockSpec(memory_space=pl.ANY),
            pl.BlockSpec(memory_space=pl.ANY),
            _resident((1, d)),
        ],
        out_specs=pl.BlockSpec((1, tm, d), lambda b, s: (b, s, 0)),
        out_shape=jax.ShapeDtypeStruct((batch, seq, d), x.dtype),
        scratch_shapes=[
            pltpu.VMEM((d // 2, w_in.shape[2]), u32),
            pltpu.VMEM((w_out_a.shape[1] // 2, d), u32),
            pltpu.VMEM((w_out_b.shape[1] // 2, d), u32),
            pltpu.VMEM((width_x // 2, d), u32),
            pltpu.VMEM((d // 2, d), u32),
            pltpu.VMEM((2, d, CAST_COLS), f32),
            pltpu.SemaphoreType.DMA((2,)),
            pltpu.VMEM((HALO_A + tm, d), f32),
            pltpu.VMEM((HALO_B + tm, d), f32),
            pltpu.VMEM((tm, d), f32),
            pltpu.VMEM((tm, d), f32),
        ],
        compiler_params=pltpu.CompilerParams(
            dimension_semantics=("arbitrary", "arbitrary"),
            vmem_limit_bytes=VMEM_LIMIT_BYTES),
        name="fused_block",
    )(x, kt, v, norm_g[0][None], w_in[0], _repeat_sublanes(conv_a_w[0]), w_out_a[0],
      _repeat_sublanes(conv_b_w[0]), conv_b_b[0][None], ln_b_g[0][None], ln_b_b[0][None],
      w_out_b[0], w_out_x[0], w_o[0], final_g[None])
```

```python
import functools
import math

import jax
import jax.numpy as jnp
from jax import lax
from jax.experimental import pallas as pl
from jax.experimental.pallas import tpu as pltpu

EPS = 1e-6
N_HEADS = 4
SUBLANES = 8
LANES = 128
HALO_A = 8
HALO_B = 32
CONV_ROWS = 8
SEQ_TILE = 256
CAST_COLS = 512
VMEM_LIMIT_BYTES = 58 * 1024 * 1024
NEG_LOG2_E = -math.log2(math.e)


def _sigmoid(v):
    return 1.0 / (1.0 + jnp.exp2(v * NEG_LOG2_E))


def _silu(v):
    return v * _sigmoid(v)


def _rms(v, g):
    ms = jnp.mean(v * v, axis=-1, keepdims=True)
    return v * lax.rsqrt(ms + EPS) * g


def _pack_rows(w):
    return pltpu.bitcast(w.astype(jnp.bfloat16), jnp.uint32)


def _stage_weight(src_hbm, dst, stage, sem):
    n_chunks = src_hbm.shape[1] // CAST_COLS

    def chunk_cols(c):
        return pl.ds(pl.multiple_of(c * CAST_COLS, CAST_COLS), CAST_COLS)

    def chunk_copy(c, slot):
        return pltpu.make_async_copy(src_hbm.at[:, chunk_cols(c)], stage.at[slot], sem.at[slot])

    chunk_copy(0, 0).start()

    def body(c, carry):
        slot = lax.rem(c, 2)

        @pl.when(c + 1 < n_chunks)
        def _():
            chunk_copy(c + 1, 1 - slot).start()

        chunk_copy(c, slot).wait()
        dst[:, chunk_cols(c)] = _pack_rows(stage[slot])
        return carry

    lax.fori_loop(0, n_chunks, body, 0)


def _repeat_sublanes(w):
    return jnp.broadcast_to(w[:, None, :], (w.shape[0], SUBLANES, w.shape[1]))


def _as_bf16(words):
    return pltpu.bitcast(words, jnp.bfloat16)


def _kv_kernel(mem_ref, g_ref, wkv_ref, kt_ref, v_ref, *, width, scale):
    mem_n = _rms(mem_ref[0], g_ref[...]).astype(jnp.bfloat16)
    kv = jnp.dot(mem_n, wkv_ref[...].astype(jnp.bfloat16),
                 preferred_element_type=jnp.float32)
    k = kv[:, :width] * scale
    kt_ref[0] = pltpu.bitcast(k.T.astype(jnp.bfloat16), jnp.uint32)
    v_ref[0] = pltpu.bitcast(kv[:, width:].astype(jnp.bfloat16), jnp.uint32)


def _causal_dwconv(ext_ref, w_ref, out_ref, cols, halo, tm):
    taps = w_ref.shape[0]
    c = cols.size
    gt = CONV_ROWS // SUBLANES
    ht = halo // SUBLANES
    sub = lax.broadcasted_iota(jnp.int32, (1, SUBLANES, c), 1)
    for g in range(tm // CONV_ROWS):
        e3 = ext_ref[pl.ds(g * CONV_ROWS, halo + CONV_ROWS), cols].reshape(ht + gt, SUBLANES, c)
        acc = None
        rot = e3
        for r in range(min(SUBLANES, taps)):
            if r == 0:
                s3 = e3[1:]
            else:
                rot = pltpu.roll(rot, 1, axis=1)
                s3 = jnp.where(sub < r, rot[:-1], rot[1:])
            for a in range((taps - 1 - r) // SUBLANES + 1):
                k = taps - 1 - (SUBLANES * a + r)
                j0 = ht - 1 - a
                term = s3[j0:j0 + gt] * w_ref[k, :, cols][None]
                acc = term if acc is None else acc + term
        out_ref[pl.ds(g * CONV_ROWS, CONV_ROWS), cols] = acc.reshape(CONV_ROWS, c)


def _block_kernel(x_ref, kt_ref, v_ref, norm_g_ref, w_in_hbm, conv_a_w_ref, w_out_a_hbm,
                  conv_b_w_ref, conv_b_b_ref, ln_g_ref, ln_b_ref, w_out_b_hbm, w_out_x_hbm,
                  w_o_hbm, final_g_ref, o_ref, w_in_ref, w_out_a_ref, w_out_b_ref, w_out_x_ref,
                  w_o_ref, stage, stage_sem, cx_ext, glu_ext, conv_a_out, conv_b_out, *, tm, d):
    s = pl.program_id(1)

    @pl.when((pl.program_id(0) == 0) & (s == 0))
    def _():
        for src, dst in ((w_in_hbm, w_in_ref), (w_out_a_hbm, w_out_a_ref), (w_out_b_hbm, w_out_b_ref),
                         (w_out_x_hbm, w_out_x_ref), (w_o_hbm, w_o_ref)):
            _stage_weight(src, dst, stage, stage_sem)

    @pl.when(s == 0)
    def _():
        cx_ext[pl.ds(0, HALO_A), :] = jnp.zeros((HALO_A, d), jnp.float32)
        glu_ext[pl.ds(0, HALO_B), :] = jnp.zeros((HALO_B, d), jnp.float32)

    x = x_ref[0]
    u = _rms(x, norm_g_ref[...]).astype(jnp.bfloat16)

    def proj(group):
        w = _as_bf16(w_in_ref[:, group * d:(group + 1) * d])
        return jnp.dot(u, w, preferred_element_type=jnp.float32)

    def out_proj(v, w_ref):
        return jnp.dot(v.astype(jnp.bfloat16), _as_bf16(w_ref[...]),
                       preferred_element_type=jnp.float32)

    conv_b_pieces = iter(range(d // LANES))

    def conv_b_piece():
        cols = pl.ds(next(conv_b_pieces) * LANES, LANES)
        _causal_dwconv(glu_ext, conv_b_w_ref, conv_b_out, cols, HALO_B, tm)

    glu_ext[pl.ds(HALO_B, tm), :] = proj(4) * _sigmoid(proj(5))

    szb = _silu(proj(6))
    conv_b_piece()
    sgb = _sigmoid(proj(10))
    conv_b_piece()

    cx_ext[pl.ds(HALO_A, tm), :] = proj(1) * proj(2)
    conv_b_piece()
    for c in range(d // LANES):
        _causal_dwconv(cx_ext, conv_a_w_ref, conv_a_out, pl.ds(c * LANES, LANES), HALO_A, tm)
    cx_ext[pl.ds(0, HALO_A), :] = cx_ext[pl.ds(tm, HALO_A), :]
    ya_in = _silu(proj(3)) * conv_a_out[...]
    conv_b_piece()
    ya = out_proj(ya_in * proj(0), w_out_a_ref)
    conv_b_piece()
    merged = _sigmoid(proj(9)) * ya
    conv_b_piece()

    q = proj(7).astype(jnp.bfloat16)
    conv_b_piece()
    hd = d // N_HEADS
    heads = []
    for h in range(N_HEADS):
        kt_h = _as_bf16(kt_ref[0, h * hd // 2:(h + 1) * hd // 2, :])
        v_h = _as_bf16(v_ref[0, :, h * hd:(h + 1) * hd])
        sc = jnp.dot(q[:, h * hd:(h + 1) * hd], kt_h, preferred_element_type=jnp.float32)
        e = jnp.exp(sc - jnp.max(sc, axis=-1, keepdims=True))
        p = e / jnp.sum(e, axis=-1, keepdims=True)
        heads.append(jnp.dot(p.astype(jnp.bfloat16), v_h, preferred_element_type=jnp.float32))
    attn = jnp.concatenate(heads, axis=-1)
    szx = _silu(proj(8))
    conv_b_piece()
    assert next(conv_b_pieces, None) is None
    glu_ext[pl.ds(0, HALO_B), :] = glu_ext[pl.ds(tm, HALO_B), :]
    yx = out_proj(szx * attn, w_out_x_ref)
    merged = merged + _sigmoid(proj(11)) * yx

    conv_b = conv_b_out[...] + conv_b_b_ref[...]
    mu = jnp.mean(conv_b, axis=-1, keepdims=True)
    xc = conv_b - mu
    var = jnp.mean(xc * xc, axis=-1, keepdims=True)
    ln = xc * lax.rsqrt(var + EPS) * ln_g_ref[...] + ln_b_ref[...]
    yb = out_proj(szb * _silu(ln), w_out_b_ref)
    merged = merged + sgb * yb

    h_new = x + out_proj(merged, w_o_ref)
    o_ref[0] = _rms(h_new, final_g_ref[...])


def _resident(shape):
    nd = len(shape)
    return pl.BlockSpec(shape, lambda b, s: (0,) * nd, pipeline_mode=pl.Buffered(1))


def kernel(x, mem, norm_g, w_in, conv_a_w, w_out_a, conv_b_w, conv_b_b, ln_b_g, ln_b_b,
           w_out_b, mem_norm_g, w_kv, w_out_x, w_o, final_g):
    batch, seq, d = x.shape
    mem_len = mem.shape[1]
    depth = w_in.shape[0]
    width_x = w_out_x.shape[1]
    taps_a = conv_a_w.shape[1]
    taps_b = conv_b_w.shape[1]
    tm = SEQ_TILE
    assert seq % tm == 0 and tm % CONV_ROWS == 0
    assert taps_a - 1 <= HALO_A and taps_b - 1 <= HALO_B
    assert depth == 1
    scale = (width_x // N_HEADS) ** -0.5
    u32 = jnp.uint32
    f32 = jnp.float32

    kt, v = pl.pallas_call(
        functools.partial(_kv_kernel, width=width_x, scale=scale),
        grid=(batch,),
        in_specs=[
            pl.BlockSpec((1, mem_len, d), lambda b: (b, 0, 0)),
            pl.BlockSpec((1, d), lambda b: (0, 0)),
            pl.BlockSpec((d, 2 * width_x), lambda b: (0, 0)),
        ],
        out_specs=[
            pl.BlockSpec((1, width_x // 2, mem_len), lambda b: (b, 0, 0)),
            pl.BlockSpec((1, mem_len // 2, width_x), lambda b: (b, 0, 0)),
        ],
        out_shape=[
            jax.ShapeDtypeStruct((batch, width_x // 2, mem_len), u32),
            jax.ShapeDtypeStruct((batch, mem_len // 2, width_x), u32),
        ],
        compiler_params=pltpu.CompilerParams(dimension_semantics=("arbitrary",)),
        name="mem_kv",
    )(mem, mem_norm_g[0][None], w_kv[0])

    return pl.pallas_call(
        functools.partial(_block_kernel, tm=tm, d=d),
        grid=(batch, seq // tm),
        in_specs=[
            pl.BlockSpec((1, tm, d), lambda b, s: (b, s, 0)),
            pl.BlockSpec((1, width_x // 2, mem_len), lambda b, s: (b, 0, 0)),
            pl.BlockSpec((1, mem_len // 2, width_x), lambda b, s: (b, 0, 0)),
            _resident((1, d)),
            pl.BlockSpec(memory_space=pl.ANY),
            _resident((taps_a, SUBLANES, d)),
            pl.BlockSpec(memory_space=pl.ANY),
            _resident((taps_b, SUBLANES, d)),
            _resident((1, d)),
            _resident((1, d)),
            _resident((1, d)),
            pl.BlockSpec(memory_space=pl.ANY),
            pl.BlockSpec(memory_space=pl.ANY),
            pl.BlockSpec(memory_space=pl.ANY),
            _resident((1, d)),
        ],
        out_specs=pl.BlockSpec((1, tm, d), lambda b, s: (b, s, 0)),
        out_shape=jax.ShapeDtypeStruct((batch, seq, d), x.dtype),
        scratch_shapes=[
            pltpu.VMEM((d // 2, w_in.shape[2]), u32),
            pltpu.VMEM((w_out_a.shape[1] // 2, d), u32),
            pltpu.VMEM((w_out_b.shape[1] // 2, d), u32),
            pltpu.VMEM((width_x // 2, d), u32),
            pltpu.VMEM((d // 2, d), u32),
            pltpu.VMEM((2, d, CAST_COLS), f32),
            pltpu.SemaphoreType.DMA((2,)),
            pltpu.VMEM((HALO_A + tm, d), f32),
            pltpu.VMEM((HALO_B + tm, d), f32),
            pltpu.VMEM((tm, d), f32),
            pltpu.VMEM((tm, d), f32),
        ],
        compiler_params=pltpu.CompilerParams(
            dimension_semantics=("arbitrary", "arbitrary"),
            vmem_limit_bytes=VMEM_LIMIT_BYTES),
        name="fused_block",
    )(x, kt, v, norm_g[0][None], w_in[0], _repeat_sublanes(conv_a_w[0]), w_out_a[0],
      _repeat_sublanes(conv_b_w[0]), conv_b_b[0][None], ln_b_g[0][None], ln_b_b[0][None],
      w_out_b[0], w_out_x[0], w_o[0], final_g[None])
```

```python
import functools
import math

import jax
import jax.numpy as jnp
from jax import lax
from jax.experimental import pallas as pl
from jax.experimental.pallas import tpu as pltpu

EPS = 1e-6
N_HEADS = 4
SUBLANES = 8
LANES = 128
HALO_A = 8
HALO_B = 32
CONV_ROWS = 8
SEQ_TILE = 256
CAST_COLS = 512
VMEM_LIMIT_BYTES = 58 * 1024 * 1024
NEG_LOG2_E = -math.log2(math.e)


def _sigmoid(v):
    return 1.0 / (1.0 + jnp.exp2(v * NEG_LOG2_E))


def _silu(v):
    return v * _sigmoid(v)


def _rms(v, g):
    ms = jnp.mean(v * v, axis=-1, keepdims=True)
    return v * lax.rsqrt(ms + EPS) * g


def _pack_rows(w):
    return pltpu.bitcast(w.astype(jnp.bfloat16), jnp.uint32)


def _stage_weight(src_hbm, dst, stage, sem):
    n_chunks = src_hbm.shape[1] // CAST_COLS

    def chunk_cols(c):
        return pl.ds(pl.multiple_of(c * CAST_COLS, CAST_COLS), CAST_COLS)

    def chunk_copy(c, slot):
        return pltpu.make_async_copy(src_hbm.at[:, chunk_cols(c)], stage.at[slot], sem.at[slot])

    chunk_copy(0, 0).start()

    def body(c, carry):
        slot = lax.rem(c, 2)

        @pl.when(c + 1 < n_chunks)
        def _():
            chunk_copy(c + 1, 1 - slot).start()

        chunk_copy(c, slot).wait()
        dst[:, chunk_cols(c)] = _pack_rows(stage[slot])
        return carry

    lax.fori_loop(0, n_chunks, body, 0)


def _repeat_sublanes(w):
    return jnp.broadcast_to(w[:, None, :], (w.shape[0], SUBLANES, w.shape[1]))


def _as_bf16(words):
    return pltpu.bitcast(words, jnp.bfloat16)


def _kv_kernel(mem_ref, g_ref, wkv_ref, kt_ref, v_ref, *, width, scale):
    mem_n = _rms(mem_ref[0], g_ref[...]).astype(jnp.bfloat16)
    kv = jnp.dot(mem_n, wkv_ref[...].astype(jnp.bfloat16),
                 preferred_element_type=jnp.float32)
    k = kv[:, :width] * scale
    kt_ref[0] = pltpu.bitcast(k.T.astype(jnp.bfloat16), jnp.uint32)
    v_ref[0] = pltpu.bitcast(kv[:, width:].astype(jnp.bfloat16), jnp.uint32)


def _causal_dwconv(ext_ref, w_ref, out_ref, cols, halo, tm):
    taps = w_ref.shape[0]
    c = cols.size
    gt = CONV_ROWS // SUBLANES
    ht = halo // SUBLANES
    sub = lax.broadcasted_iota(jnp.int32, (1, SUBLANES, c), 1)
    for g in range(tm // CONV_ROWS):
        e3 = ext_ref[pl.ds(g * CONV_ROWS, halo + CONV_ROWS), cols].reshape(ht + gt, SUBLANES, c)
        acc = None
        rot = e3
        for r in range(min(SUBLANES, taps)):
            if r == 0:
                s3 = e3[1:]
            else:
                rot = pltpu.roll(rot, 1, axis=1)
                s3 = jnp.where(sub < r, rot[:-1], rot[1:])
            for a in range((taps - 1 - r) // SUBLANES + 1):
                k = taps - 1 - (SUBLANES * a + r)
                j0 = ht - 1 - a
                term = s3[j0:j0 + gt] * w_ref[k, :, cols][None]
                acc = term if acc is None else acc + term
        out_ref[pl.ds(g * CONV_ROWS, CONV_ROWS), cols] = acc.reshape(CONV_ROWS, c)


def _block_kernel(x_ref, xlag_ref, kt_ref, v_ref, norm_g_ref, w_in_hbm, conv_a_w_ref, w_out_a_hbm,
                  conv_b_w_ref, conv_b_b_ref, ln_g_ref, ln_b_ref, w_out_b_hbm, w_out_x_hbm,
                  w_o_hbm, final_g_ref, o_ref, w_in_ref, w_out_a_ref, w_out_b_ref, w_out_x_ref,
                  w_o_ref, stage, stage_sem, cx_ext, glu_ext, conv_a_out, conv_b_out, merged_buf, *, tm, d,
                  tiles_per_batch):
    n = pl.program_id(0)
    s = lax.rem(n, tiles_per_batch)

    @pl.when(n == 0)
    def _():
        merged_buf[...] = jnp.zeros(merged_buf.shape, merged_buf.dtype)
        for src, dst in ((w_in_hbm, w_in_ref), (w_out_a_hbm, w_out_a_ref), (w_out_b_hbm, w_out_b_ref),
                         (w_out_x_hbm, w_out_x_ref), (w_o_hbm, w_o_ref)):
            _stage_weight(src, dst, stage, stage_sem)

    @pl.when(s == 0)
    def _():
        cx_ext[pl.ds(0, HALO_A), :] = jnp.zeros((HALO_A, d), jnp.float32)
        glu_ext[pl.ds(0, HALO_B), :] = jnp.zeros((HALO_B, d), jnp.float32)

    h_prev = xlag_ref[...] + jnp.dot(merged_buf[...], _as_bf16(w_o_ref[...]),
                                     preferred_element_type=jnp.float32)
    o_ref[...] = _rms(h_prev, final_g_ref[...])

    x = x_ref[...]
    u = _rms(x, norm_g_ref[...]).astype(jnp.bfloat16)

    def proj(group):
        w = _as_bf16(w_in_ref[:, group * d:(group + 1) * d])
        return jnp.dot(u, w, preferred_element_type=jnp.float32)

    def out_proj(v, w_ref):
        return jnp.dot(v.astype(jnp.bfloat16), _as_bf16(w_ref[...]),
                       preferred_element_type=jnp.float32)

    conv_b_pieces = iter(range(d // LANES))

    def conv_b_piece():
        cols = pl.ds(next(conv_b_pieces) * LANES, LANES)
        _causal_dwconv(glu_ext, conv_b_w_ref, conv_b_out, cols, HALO_B, tm)

    glu_ext[pl.ds(HALO_B, tm), :] = proj(4) * _sigmoid(proj(5))

    szb = _silu(proj(6))
    conv_b_piece()
    sgb = _sigmoid(proj(10))
    conv_b_piece()

    cx_ext[pl.ds(HALO_A, tm), :] = proj(1) * proj(2)
    conv_b_piece()
    for c in range(d // LANES):
        _causal_dwconv(cx_ext, conv_a_w_ref, conv_a_out, pl.ds(c * LANES, LANES), HALO_A, tm)
    cx_ext[pl.ds(0, HALO_A), :] = cx_ext[pl.ds(tm, HALO_A), :]
    ya_in = _silu(proj(3)) * conv_a_out[...]
    conv_b_piece()
    ya = out_proj(ya_in * proj(0), w_out_a_ref)
    conv_b_piece()
    merged = _sigmoid(proj(9)) * ya
    conv_b_piece()

    q = proj(7).astype(jnp.bfloat16)
    conv_b_piece()
    hd = d // N_HEADS
    heads = []
    for h in range(N_HEADS):
        kt_h = _as_bf16(kt_ref[0, h * hd // 2:(h + 1) * hd // 2, :])
        v_h = _as_bf16(v_ref[0, :, h * hd:(h + 1) * hd])
        sc = jnp.dot(q[:, h * hd:(h + 1) * hd], kt_h, preferred_element_type=jnp.float32)
        e = jnp.exp(sc - jnp.max(sc, axis=-1, keepdims=True))
        p = e / jnp.sum(e, axis=-1, keepdims=True)
        heads.append(jnp.dot(p.astype(jnp.bfloat16), v_h, preferred_element_type=jnp.float32))
    attn = jnp.concatenate(heads, axis=-1)
    szx = _silu(proj(8))
    conv_b_piece()
    assert next(conv_b_pieces, None) is None
    glu_ext[pl.ds(0, HALO_B), :] = glu_ext[pl.ds(tm, HALO_B), :]
    yx = out_proj(szx * attn, w_out_x_ref)
    merged = merged + _sigmoid(proj(11)) * yx

    conv_b = conv_b_out[...] + conv_b_b_ref[...]
    mu = jnp.mean(conv_b, axis=-1, keepdims=True)
    xc = conv_b - mu
    var = jnp.mean(xc * xc, axis=-1, keepdims=True)
    ln = xc * lax.rsqrt(var + EPS) * ln_g_ref[...] + ln_b_ref[...]
    yb = out_proj(szb * _silu(ln), w_out_b_ref)
    merged = merged + sgb * yb

    merged_buf[...] = merged.astype(jnp.bfloat16)


def _resident(shape):
    nd = len(shape)
    return pl.BlockSpec(shape, lambda n: (0,) * nd, pipeline_mode=pl.Buffered(1))


def kernel(x, mem, norm_g, w_in, conv_a_w, w_out_a, conv_b_w, conv_b_b, ln_b_g, ln_b_b,
           w_out_b, mem_norm_g, w_kv, w_out_x, w_o, final_g):
    batch, seq, d = x.shape
    mem_len = mem.shape[1]
    depth = w_in.shape[0]
    width_x = w_out_x.shape[1]
    taps_a = conv_a_w.shape[1]
    taps_b = conv_b_w.shape[1]
    tm = SEQ_TILE
    assert seq % tm == 0 and tm % CONV_ROWS == 0
    assert taps_a - 1 <= HALO_A and taps_b - 1 <= HALO_B
    assert depth == 1
    scale = (width_x // N_HEADS) ** -0.5
    u32 = jnp.uint32
    f32 = jnp.float32

    kt, v = pl.pallas_call(
        functools.partial(_kv_kernel, width=width_x, scale=scale),
        grid=(batch,),
        in_specs=[
            pl.BlockSpec((1, mem_len, d), lambda b: (b, 0, 0)),
            pl.BlockSpec((1, d), lambda b: (0, 0)),
            pl.BlockSpec((d, 2 * width_x), lambda b: (0, 0)),
        ],
        out_specs=[
            pl.BlockSpec((1, width_x // 2, mem_len), lambda b: (b, 0, 0)),
            pl.BlockSpec((1, mem_len // 2, width_x), lambda b: (b, 0, 0)),
        ],
        out_shape=[
            jax.ShapeDtypeStruct((batch, width_x // 2, mem_len), u32),
            jax.ShapeDtypeStruct((batch, mem_len // 2, width_x), u32),
        ],
        compiler_params=pltpu.CompilerParams(dimension_semantics=("arbitrary",)),
        name="mem_kv",
    )(mem, mem_norm_g[0][None], w_kv[0])

    tiles_per_batch = seq // tm
    n_tiles = batch * tiles_per_batch

    def cur(n):
        return jnp.minimum(n, n_tiles - 1)

    def lag(n):
        return jnp.maximum(n - 1, 0)

    x_rows = x.reshape(batch * seq, d)
    out = pl.pallas_call(
        functools.partial(_block_kernel, tm=tm, d=d, tiles_per_batch=tiles_per_batch),
        grid=(n_tiles + 1,),
        in_specs=[
            pl.BlockSpec((tm, d), lambda n: (cur(n), 0)),
            pl.BlockSpec((tm, d), lambda n: (lag(n), 0)),
            pl.BlockSpec((1, width_x // 2, mem_len), lambda n: (cur(n) // tiles_per_batch, 0, 0)),
            pl.BlockSpec((1, mem_len // 2, width_x), lambda n: (cur(n) // tiles_per_batch, 0, 0)),
            _resident((1, d)),
            pl.BlockSpec(memory_space=pl.ANY),
            _resident((taps_a, SUBLANES, d)),
            pl.BlockSpec(memory_space=pl.ANY),
            _resident((taps_b, SUBLANES, d)),
            _resident((1, d)),
            _resident((1, d)),
            _resident((1, d)),
            pl.BlockSpec(memory_space=pl.ANY),
            pl.BlockSpec(memory_space=pl.ANY),
            pl.BlockSpec(memory_space=pl.ANY),
            _resident((1, d)),
        ],
        out_specs=pl.BlockSpec((tm, d), lambda n: (lag(n), 0)),
        out_shape=jax.ShapeDtypeStruct((batch * seq, d), x.dtype),
        scratch_shapes=[
            pltpu.VMEM((d // 2, w_in.shape[2]), u32),
            pltpu.VMEM((w_out_a.shape[1] // 2, d), u32),
            pltpu.VMEM((w_out_b.shape[1] // 2, d), u32),
            pltpu.VMEM((width_x // 2, d), u32),
            pltpu.VMEM((d // 2, d), u32),
            pltpu.VMEM((2, d, CAST_COLS), f32),
            pltpu.SemaphoreType.DMA((2,)),
            pltpu.VMEM((HALO_A + tm, d), f32),
            pltpu.VMEM((HALO_B + tm, d), f32),
            pltpu.VMEM((tm, d), f32),
            pltpu.VMEM((tm, d), f32),
            pltpu.VMEM((tm, d), jnp.bfloat16),
        ],
        compiler_params=pltpu.CompilerParams(
            dimension_semantics=("arbitrary",),
            vmem_limit_bytes=VMEM_LIMIT_BYTES),
        name="fused_block",
    )(x_rows, x_rows, kt, v, norm_g[0][None], w_in[0], _repeat_sublanes(conv_a_w[0]), w_out_a[0],
      _repeat_sublanes(conv_b_w[0]), conv_b_b[0][None], ln_b_g[0][None], ln_b_b[0][None],
      w_out_b[0], w_out_x[0], w_o[0], final_g[None])
    return out.reshape(batch, seq, d)
```
